```python
import jax, jax.numpy as jnp
from jax import lax
import numpy as np

D_MODEL = 1024
BATCH = 1
SEQ = 16384
DEPTH = 2

GRID_W = 64
CTX_LEN = 256
N_EVEN = (DEPTH + 1) // 2
N_ODD = DEPTH // 2
EPS = 1e-6
ADA_SCALE = 0.3

CONV_CH = D_MODEL // 2
CONV_W = 3
HEAD_DIM = 64
N_Q_HEADS = (D_MODEL // 2) // HEAD_DIM
N_KV_HEADS = 2
GQA_GROUP = N_Q_HEADS // N_KV_HEADS
Q_DIM = N_Q_HEADS * HEAD_DIM
KV_DIM = N_KV_HEADS * HEAD_DIM
WINDOW = 128
ATT_BLOCK = 128
ROPE_BASE = 10000.0
ROPE_PAIRS = HEAD_DIM // 4
AB_SPLITS = [CONV_CH, 2 * CONV_CH, 3 * CONV_CH, 3 * CONV_CH + Q_DIM, 3 * CONV_CH + Q_DIM + KV_DIM]
KV_OFF = 3 * CONV_CH + Q_DIM
AB_IN = 3 * CONV_CH + Q_DIM + 2 * KV_DIM
AB_OUT = CONV_CH + Q_DIM
D_FF = 2816
D_INNER = 2 * D_MODEL
SSD_HEADDIM = 64
SSD_HEADS = D_INNER // SSD_HEADDIM
SSD_GROUPS = 4
SSD_REP = SSD_HEADS // SSD_GROUPS
SSD_STATE = 128
SSD_CHUNK = 128
SSD_CONV_W = 3
GN_DIM = SSD_GROUPS * SSD_STATE
XB_DIM = D_INNER + GN_DIM
SSD_CONV_DIM = D_INNER + 2 * GN_DIM
DT_OFF = D_INNER + SSD_CONV_DIM
SSD_IN = DT_OFF + 2 * SSD_HEADS
N_EXPERTS = 8
TOP_K = 2
D_FF_EXPERT = 3584

kernel_name = "hybrid_conv_swa_ssd_moe_dit_block"


def rmsnorm(x, g):
    xf = x.astype(jnp.float32)
    y = xf * lax.rsqrt(jnp.mean(xf * xf, axis=-1, keepdims=True) + EPS)
    return y.astype(x.dtype) * g


def modulate(x, g, shift, scale):
    return rmsnorm(x, g) * (1 + scale) + shift


def conv_centered(u, w, bias=None):
    k_w = w.shape[0]
    pad = k_w // 2
    L = u.shape[1]
    up = jnp.pad(u, ((0, 0), (pad, pad), (0, 0)))
    y = up[:, 0:L] * w[0]
    for k in range(1, k_w):
        y = y + up[:, k:k + L] * w[k]
    if bias is not None:
        y = y + bias
    return y


def heads(t, n):
    return t.reshape(t.shape[:2] + (n, HEAD_DIM))


def rope2d_tables(L):
    rows = L // GRID_W
    row = jnp.repeat(jnp.arange(rows, dtype=jnp.float32), GRID_W)
    col = jnp.tile(jnp.arange(GRID_W, dtype=jnp.float32), rows)
    inv = ROPE_BASE ** (-jnp.arange(ROPE_PAIRS, dtype=jnp.float32) / ROPE_PAIRS)
    ang = jnp.stack([row, col], axis=-1)[..., None] * inv
    return jnp.cos(ang), jnp.sin(ang)


def apply_rope2d(x, cos, sin):
    b, L, H, _ = x.shape
    xr = x.reshape(b, L, H, 2, 2, ROPE_PAIRS)
    x1, x2 = xr[..., 0, :], xr[..., 1, :]
    cs, sn = cos[:, None], sin[:, None]
    out = jnp.stack([x1 * cs - x2 * sn, x2 * cs + x1 * sn], axis=-2)
    return out.reshape(x.shape).astype(x.dtype)


def short_conv_mixer(g_b, g_c, h_a, conv_w):
    return g_b * conv_centered(g_c * h_a, conv_w)


def sink_column(sink, shape):
    s = sink.astype(jnp.float32).reshape((1,) * (len(shape) - 4) + (N_KV_HEADS, GQA_GROUP, 1, 1))
    return jnp.broadcast_to(s, shape[:-1] + (1,))


def context_attention(q, k, v, sink):
    b, lc = q.shape[:2]
    qg = q.reshape(b, lc, N_KV_HEADS, GQA_GROUP, HEAD_DIM)
    s = jnp.einsum('bqhgd,bkhd->bhgqk', qg, k).astype(jnp.float32) * (HEAD_DIM ** -0.5)
    s = jnp.concatenate([s, sink_column(sink, s.shape)], axis=-1)
    p = jax.nn.softmax(s, axis=-1)[..., :-1].astype(v.dtype)
    return jnp.einsum('bhgqk,bkhd->bqhgd', p, v).reshape(b, lc, Q_DIM)


def windowed_attention(q, k, v, kc, vc, sink):
    b, L = q.shape[:2]
    lc = kc.shape[1]
    nb = L // ATT_BLOCK
    qb = q.reshape(b, nb, ATT_BLOCK, N_KV_HEADS, GQA_GROUP, HEAD_DIM)

    def band(t):
        tp = jnp.pad(t, ((0, 0), (ATT_BLOCK, ATT_BLOCK), (0, 0), (0, 0)))
        tp = tp.reshape(b, nb + 2, ATT_BLOCK, N_KV_HEADS, HEAD_DIM)
        return jnp.concatenate([tp[:, :-2], tp[:, 1:-1], tp[:, 2:]], axis=2)

    kb, vb = band(k), band(v)
    scale = HEAD_DIM ** -0.5
    s_loc = jnp.einsum('bnqhgd,bnkhd->bnhgqk', qb, kb).astype(jnp.float32) * scale
    s_ctx = jnp.einsum('bnqhgd,bchd->bnhgqc', qb, kc).astype(jnp.float32) * scale
    blk = jnp.arange(nb)[:, None, None]
    qpos = blk * ATT_BLOCK + jnp.arange(ATT_BLOCK)[None, :, None]
    kpos = (blk - 1) * ATT_BLOCK + jnp.arange(3 * ATT_BLOCK)[None, None, :]
    valid = (jnp.abs(qpos - kpos) <= WINDOW) & (kpos >= 0) & (kpos < L)
    s_loc = jnp.where(valid[None, :, None, None], s_loc, -jnp.inf)
    logits = jnp.concatenate([s_loc, s_ctx, sink_column(sink, s_loc.shape)], axis=-1)
    p = jax.nn.softmax(logits, axis=-1).astype(v.dtype)
    n_loc = 3 * ATT_BLOCK
    o = (jnp.einsum('bnhgqk,bnkhd->bnqhgd', p[..., :n_loc], vb)
         + jnp.einsum('bnhgqc,bchd->bnqhgd', p[..., n_loc:n_loc + lc], vc))
    return o.reshape(b, L, Q_DIM)


def mixer_ab(h, hc, w_in, conv_w, sink, w_out, cos, sin, last):
    if last:
        kc, vc = jnp.split(hc @ w_in[:, KV_OFF:], [KV_DIM], axis=-1)
    else:
        gbc, gcc, hac, qc, kc, vc = jnp.split(hc @ w_in, AB_SPLITS, axis=-1)
    kc, vc = heads(kc, N_KV_HEADS), heads(vc, N_KV_HEADS)
    g_b, g_c, h_a, q, k, v = jnp.split(h @ w_in, AB_SPLITS, axis=-1)
    a_out = short_conv_mixer(g_b, g_c, h_a, conv_w)
    q = apply_rope2d(heads(q, N_Q_HEADS), cos, sin)
    k = apply_rope2d(heads(k, N_KV_HEADS), cos, sin)
    att = windowed_attention(q, k, heads(v, N_KV_HEADS), kc, vc, sink)
    y = jnp.concatenate([a_out, att], axis=-1) @ w_out
    if last:
        return y, None
    ac = short_conv_mixer(gbc, gcc, hac, conv_w)
    attc = context_attention(heads(qc, N_Q_HEADS), kc, vc, sink)
    yc = jnp.concatenate([ac, attc], axis=-1) @ w_out
    return y, yc


def split_xb(xb):
    b, L = xb.shape[:2]
    xs, bm = jnp.split(xb, [D_INNER], axis=-1)
    return xs.reshape(b, L, SSD_HEADS, SSD_HEADDIM), bm.reshape(b, L, SSD_GROUPS, SSD_STATE)


def dt_from(dt_raw, bias_f, bias_b):
    d = dt_raw.astype(jnp.float32)
    return (jax.nn.softplus(d[..., :SSD_HEADS] + bias_f.astype(jnp.float32)),
            jax.nn.softplus(d[..., SSD_HEADS:] + bias_b.astype(jnp.float32)))


def ssd_features(p, conv_w, conv_b, dtb_f, dtb_b):
    z, xbc, dt_raw = jnp.split(p, [D_INNER, DT_OFF], axis=-1)
    xbc = jax.nn.silu(conv_centered(xbc, conv_w, conv_b))
    xs, bm = split_xb(xbc[..., :XB_DIM])
    cm = xbc[..., XB_DIM:].reshape(p.shape[:2] + (SSD_GROUPS, SSD_STATE))
    dt_f, dt_b = dt_from(dt_raw, dtb_f, dtb_b)
    return z, xs, bm, cm, dt_f, dt_b


def ssd_scan(x, dt, a_coef, bm, cm, init):
    f32 = jnp.float32
    b, L = x.shape[:2]
    nc, q = L // SSD_CHUNK, SSD_CHUNK
    a = (dt * a_coef).reshape(b, nc, q, SSD_GROUPS, SSD_REP).transpose(0, 1, 3, 4, 2)
    a_cum = jnp.cumsum(a, axis=-1)
    xdt = (x.astype(f32) * dt[..., None]).reshape(b, nc, q, SSD_GROUPS, SSD_REP, SSD_HEADDIM)
    bc = bm.astype(f32).reshape(b, nc, q, SSD_GROUPS, SSD_STATE)
    cc = cm.astype(f32).reshape(b, nc, q, SSD_GROUPS, SSD_STATE)
    causal = jnp.tril(jnp.ones((q, q), dtype=bool))
    lmat = jnp.exp(jnp.where(causal, a_cum[..., :, None] - a_cum[..., None, :], -jnp.inf))
    cb = jnp.einsum('bclgn,bcsgn->bcgls', cc, bc)
    y_diag = jnp.einsum('bcgrls,bcsgrp->bclgrp', cb[:, :, :, None] * lmat, xdt)
    decay_end = jnp.exp(a_cum[..., -1:] - a_cum).transpose(0, 1, 4, 2, 3)
    chunk_states = jnp.einsum('bclgn,bclgrp->bcgrpn', bc, xdt * decay_end[..., None])
    chunk_decay = jnp.exp(a_cum[..., -1])

    def step(state, inp):
        dec, st = inp
        return state * dec[..., None, None] + st, state

    final, prev = lax.scan(step, init.astype(f32),
                           (jnp.moveaxis(chunk_decay, 1, 0), jnp.moveaxis(chunk_states, 1, 0)))
    prev = jnp.moveaxis(prev, 0, 1)
    decay_in = jnp.exp(a_cum).transpose(0, 1, 4, 2, 3)[..., None]
    y_off = jnp.einsum('bclgn,bcgrpn->bclgrp', cc, prev) * decay_in
    return (y_diag + y_off).reshape(b, L, SSD_HEADS, SSD_HEADDIM), final


def ssd_final_state(x, dt, a_coef, bm):
    b, L = x.shape[:2]
    a_cum = jnp.cumsum(dt * a_coef, axis=1)
    w = dt * jnp.exp(a_cum[:, -1:] - a_cum)
    xw = (x.astype(jnp.float32) * w[..., None]).reshape(b, L, SSD_GROUPS, SSD_REP, SSD_HEADDIM)
    return jnp.einsum('blgn,blgrp->bgrpn', bm.astype(jnp.float32), xw)


def flip(t):
    return jnp.flip(t, axis=1)


def ssd_bidirectional(xs, bm, cm, dt_f, dt_b, a_f, a_b, init_f, init_b):
    y_f, s_f = ssd_scan(xs, dt_f, a_f, bm, cm, init_f)
    y_b, s_b = ssd_scan(flip(xs), flip(dt_b), a_b, flip(bm), flip(cm), init_b)
    return y_f + flip(y_b), s_f, s_b


def ssd_output(y, xs, z, d_skip, norm_g, w_out):
    b, L = y.shape[:2]
    y = y + xs.astype(jnp.float32) * d_skip.astype(jnp.float32)[:, None]
    gshape = (b, L, SSD_GROUPS, D_INNER // SSD_GROUPS)
    y = y.reshape(gshape) * jax.nn.silu(z.astype(jnp.float32)).reshape(gshape)
    y = y * lax.rsqrt(jnp.mean(y * y, axis=-1, keepdims=True) + EPS)
    return (y.reshape(b, L, D_INNER).astype(z.dtype) * norm_g) @ w_out


def mixer_ssd(h, hc, w_in, conv_w, conv_b, dtb_f, dtb_b, alog_f, alog_b, d_skip, norm_g, w_out, last):
    a_f = -jnp.exp(alog_f.astype(jnp.float32))
    a_b = -jnp.exp(alog_b.astype(jnp.float32))
    b = h.shape[0]
    if last:
        xb_c = jax.nn.silu(conv_centered(hc @ w_in[:, D_INNER:D_INNER + XB_DIM],
                                         conv_w[:, :XB_DIM], conv_b[:XB_DIM]))
        xs_c, b_c = split_xb(xb_c)
        dtf_c, dtb_c = dt_from(hc @ w_in[:, DT_OFF:], dtb_f, dtb_b)
        s_f = ssd_final_state(xs_c, dtf_c, a_f, b_c)
        s_b = ssd_final_state(flip(xs_c), flip(dtb_c), a_b, flip(b_c))
        yc = None
    else:
        z_c, xs_c, b_c, c_c, dtf_c, dtb_c = ssd_features(hc @ w_in, conv_w, conv_b, dtb_f, dtb_b)
        zero = jnp.zeros((b, SSD_GROUPS, SSD_REP, SSD_HEADDIM, SSD_STATE), jnp.float32)
        y_c, s_f, s_b = ssd_bidirectional(xs_c, b_c, c_c, dtf_c, dtb_c, a_f, a_b, zero, zero)
        yc = ssd_output(y_c, xs_c, z_c, d_skip, norm_g, w_out)
    z, xs, bm, cm, dt_f, dt_b = ssd_features(h @ w_in, conv_w, conv_b, dtb_f, dtb_b)
    y, _, _ = ssd_bidirectional(xs, bm, cm, dt_f, dt_b, a_f, a_b, s_f, s_b)
    return ssd_output(y, xs, z, d_skip, norm_g, w_out), yc


def swiglu(h, wg, wu, wd):
    return (jax.nn.silu(h @ wg) * (h @ wu)) @ wd


def moe_ffn(h, router, wg, wu, wd):
    logits = (h @ router).astype(jnp.float32)
    top_v, top_i = lax.top_k(logits, TOP_K)
    w = jax.nn.softmax(top_v, axis=-1)
    gates = jnp.einsum('btk,btke->bte', w, jax.nn.one_hot(top_i, N_EXPERTS, dtype=jnp.float32)).astype(h.dtype)
    out = jnp.zeros_like(h)
    for e in range(N_EXPERTS):
        out = out + gates[..., e:e + 1] * swiglu(h, wg[e], wu[e], wd[e])
    return out


def setup_inputs(seed: int = 0) -> dict:
    key = jax.random.key(seed)
    ks = iter(jax.random.split(key, 40))
    f32 = jnp.float32

    def nrm(shape, scale):
        return jax.random.normal(next(ks), shape, f32) * scale

    def gain(shape):
        return 1.0 + nrm(shape, 0.02)

    def dt_bias(shape):
        u = jax.random.uniform(next(ks), shape, f32)
        dt = jnp.exp(u * (np.log(0.1) - np.log(0.001)) + np.log(0.001)).astype(f32)
        return dt + jnp.log(-jnp.expm1(-dt))

    def a_log(shape):
        return jnp.log(jax.random.uniform(next(ks), shape, f32, minval=1.0, maxval=16.0))

    D = D_MODEL
    return {
        "x": nrm((BATCH, SEQ, D), 1.0),
        "c": nrm((BATCH, D), 1.0),
        "ctx": nrm((BATCH, CTX_LEN, D), 1.0),
        "c_ctx": nrm((D,), 1.0),
        "ada_w": nrm((DEPTH, D, 6 * D), ADA_SCALE * D ** -0.5),
        "ada_b": nrm((DEPTH, 6 * D), 0.02),
        "norm1_g": gain((DEPTH, D)),
        "norm2_g": gain((DEPTH, D)),
        "ab_w_in": nrm((N_EVEN, D, AB_IN), D ** -0.5),
        "ab_conv_w": nrm((N_EVEN, CONV_W, CONV_CH), CONV_W ** -0.5),
        "ab_sink": nrm((N_EVEN, N_Q_HEADS), 0.5),
        "ab_w_out": nrm((N_EVEN, AB_OUT, D), AB_OUT ** -0.5),
        "ffn_w_gate": nrm((N_EVEN, D, D_FF), D ** -0.5),
        "ffn_w_up": nrm((N_EVEN, D, D_FF), D ** -0.5),
        "ffn_w_down": nrm((N_EVEN, D_FF, D), D_FF ** -0.5),
        "ssd_w_in": nrm((N_ODD, D, SSD_IN), D ** -0.5),
        "ssd_conv_w": nrm((N_ODD, SSD_CONV_W, SSD_CONV_DIM), SSD_CONV_W ** -0.5),
        "ssd_conv_b": nrm((N_ODD, SSD_CONV_DIM), 0.02),
        "ssd_dt_bias_f": dt_bias((N_ODD, SSD_HEADS)),
        "ssd_dt_bias_b": dt_bias((N_ODD, SSD_HEADS)),
        "ssd_a_log_f": a_log((N_ODD, SSD_HEADS)),
        "ssd_a_log_b": a_log((N_ODD, SSD_HEADS)),
        "ssd_d": gain((N_ODD, SSD_HEADS)),
        "ssd_norm_g": gain((N_ODD, D_INNER)),
        "ssd_w_out": nrm((N_ODD, D_INNER, D), D_INNER ** -0.5),
        "moe_router": nrm((N_ODD, D, N_EXPERTS), D ** -0.5),
        "moe_w_gate": nrm((N_ODD, N_EXPERTS, D, D_FF_EXPERT), D ** -0.5),
        "moe_w_up": nrm((N_ODD, N_EXPERTS, D, D_FF_EXPERT), D ** -0.5),
        "moe_w_down": nrm((N_ODD, N_EXPERTS, D_FF_EXPERT, D), D_FF_EXPERT ** -0.5),
        "final_g": gain((D,)),
    }


def reference(x, c, ctx, c_ctx, ada_w, ada_b, norm1_g, norm2_g, ab_w_in, ab_conv_w, ab_sink, ab_w_out,
              ffn_w_gate, ffn_w_up, ffn_w_down, ssd_w_in, ssd_conv_w, ssd_conv_b, ssd_dt_bias_f,
              ssd_dt_bias_b, ssd_a_log_f, ssd_a_log_b, ssd_d, ssd_norm_g, ssd_w_out, moe_router,
              moe_w_gate, moe_w_up, moe_w_down, final_g):
    L = x.shape[1]
    cos, sin = rope2d_tables(L)
    sc_x = jax.nn.silu(c)
    sc_c = jax.nn.silu(c_ctx)
    xc = ctx
    for i in range(DEPTH):
        last = i == DEPTH - 1
        j = i // 2
        m = (sc_x @ ada_w[i] + ada_b[i])[:, None, :]
        mc = sc_c @ ada_w[i] + ada_b[i]
        sh1, sc1, g1, sh2, sc2, g2 = jnp.split(m, 6, axis=-1)
        sh1c, sc1c, g1c, sh2c, sc2c, g2c = jnp.split(mc, 6, axis=-1)
        h = modulate(x, norm1_g[i], sh1, sc1)
        hc = modulate(xc, norm1_g[i], sh1c, sc1c)
        if i % 2 == 0:
            y, yc = mixer_ab(h, hc, ab_w_in[j], ab_conv_w[j], ab_sink[j], ab_w_out[j], cos, sin, last)
            ffn = lambda t, j=j: swiglu(t, ffn_w_gate[j], ffn_w_up[j], ffn_w_down[j])
        else:
            y, yc = mixer_ssd(h, hc, ssd_w_in[j], ssd_conv_w[j], ssd_conv_b[j], ssd_dt_bias_f[j],
                              ssd_dt_bias_b[j], ssd_a_log_f[j], ssd_a_log_b[j], ssd_d[j],
                              ssd_norm_g[j], ssd_w_out[j], last)
            ffn = lambda t, j=j: moe_ffn(t, moe_router[j], moe_w_gate[j], moe_w_up[j], moe_w_down[j])
        x = x + g1 * y
        x = x + g2 * ffn(modulate(x, norm2_g[i], sh2, sc2))
        if not last:
            xc = xc + g1c * yc
            xc = xc + g2c * ffn(modulate(xc, norm2_g[i], sh2c, sc2c))
    return rmsnorm(x, final_g)
```

```python
import functools

import jax
import jax.numpy as jnp
from jax import lax
from jax.experimental import pallas as pl
from jax.experimental.pallas import tpu as pltpu

F32 = jnp.float32
BF16 = jnp.bfloat16

EPS = 1e-6
GRID_W = 64
HEAD_DIM = 64
N_Q_HEADS = 8
N_KV_HEADS = 2
WINDOW = 128
ROPE_BASE = 10000.0
ROPE_PAIRS = HEAD_DIM // 4
CONV_CH = 512
Q_DIM = 512
KV_DIM = 128
D_INNER = 2048
SSD_HEADS = 32
SSD_GROUPS = 4
SSD_STATE = 128
SSD_CHUNK = 128
GN_DIM = SSD_GROUPS * SSD_STATE
N_EXPERTS = 8

LANES = 128
BF16_SUBLANES = 16
VMEM_LIMIT = 56 * 1024 * 1024
NEG_BIG = -1e30

Q_HEAD_ORDER = (0, 4, 1, 5, 2, 6, 3, 7)


def _params(*sem):
    return pltpu.CompilerParams(dimension_semantics=sem, vmem_limit_bytes=VMEM_LIMIT)


def _sigmoid(v):
    return 1.0 / (1.0 + jnp.exp(-v))


def _silu(v):
    return v * _sigmoid(v)


def _dot(a, b):
    return jnp.dot(a, b, preferred_element_type=F32)


def _dot_nt(a, b):
    return lax.dot_general(a, b, (((1,), (1,)), ((), ())), preferred_element_type=F32)


def _dot_tn(a, b):
    return lax.dot_general(a, b, (((0,), (0,)), ((), ())), preferred_element_type=F32)


def _split2(v):
    hi = v.astype(BF16)
    lo = (v - hi.astype(F32)).astype(BF16)
    return hi, lo


def _modulated(x, g_ref, sc_ref, sh_ref):
    r = lax.rsqrt(jnp.mean(x * x, axis=-1, keepdims=True) + EPS)
    a = g_ref[...] * (1.0 + sc_ref[...])
    return (x * r * a + sh_ref[...]).astype(BF16)


def _row(v):
    return v.reshape(1, -1)


def _full(shape):
    n = len(shape)
    return pl.BlockSpec(shape, lambda *_: (0,) * n)


def _ada_kernel(cv_ref, w_ref, b_ref, o_ref):
    s = _silu(cv_ref[...]).astype(BF16)
    o_ref[0] = _dot(s, w_ref[0].astype(BF16)) + b_ref[0]


def _ada(cv, ada_w, ada_b):
    depth, d, n = ada_w.shape
    tn = 1536
    return pl.pallas_call(
        _ada_kernel,
        grid=(depth, n // tn),
        in_specs=[pl.BlockSpec((8, d), lambda i, j: (0, 0)),
                  pl.BlockSpec((1, d, tn), lambda i, j: (i, 0, j)),
                  pl.BlockSpec((1, 1, tn), lambda i, j: (i, 0, j))],
        out_specs=pl.BlockSpec((1, 8, tn), lambda i, j: (i, 0, j)),
        out_shape=jax.ShapeDtypeStruct((depth, 8, n), F32),
        compiler_params=_params("arbitrary", "arbitrary"),
        name="ada_mod",
    )(cv, ada_w, ada_b.reshape(depth, 1, n))


def _in0_kernel(x_ref, g_ref, sc_ref, sh_ref, w_ref, cos_ref, sin_ref, pc_ref, q_ref, kv_ref):
    tm = x_ref.shape[0]
    hb = _modulated(x_ref[...], g_ref, sc_ref, sh_ref)
    for c in range(3):
        sl = slice(c * CONV_CH, (c + 1) * CONV_CH)
        pc_ref[:, sl] = _dot(hb, w_ref[:, sl]).astype(BF16)
    cs = cos_ref[...]
    sn = sin_ref[...]
    lane = lax.broadcasted_iota(jnp.int32, (tm, LANES), 1)
    first = (lane & ROPE_PAIRS) == 0

    def rope(blk):
        sw = jnp.where(first, pltpu.roll(blk, LANES - ROPE_PAIRS, 1), pltpu.roll(blk, ROPE_PAIRS, 1))
        return blk * cs + sw * sn

    q0 = 3 * CONV_CH
    q = _dot(hb, w_ref[:, q0:q0 + Q_DIM])
    for m in range(Q_DIM // LANES):
        sl = slice(m * LANES, (m + 1) * LANES)
        q_ref[:, sl] = (rope(q[:, sl]) * (HEAD_DIM ** -0.5)).astype(BF16)
    kv = _dot(hb, w_ref[:, q0 + Q_DIM:q0 + Q_DIM + 2 * KV_DIM])
    kv_ref[:, 0:KV_DIM] = rope(kv[:, 0:KV_DIM]).astype(BF16)
    kv_ref[:, KV_DIM:] = kv[:, KV_DIM:].astype(BF16)


def _in0(x, g, sc, sh, w, cos, sin, tm):
    L, d = x.shape
    n = w.shape[1]
    row = lambda i: (i, 0)
    return pl.pallas_call(
        _in0_kernel,
        grid=(L // tm,),
        in_specs=[pl.BlockSpec((tm, d), row), _full((1, d)), _full((1, d)), _full((1, d)),
                  _full((d, n)), pl.BlockSpec((tm, LANES), row), pl.BlockSpec((tm, LANES), row)],
        out_specs=[pl.BlockSpec((tm, 3 * CONV_CH), row), pl.BlockSpec((tm, Q_DIM), row),
                   pl.BlockSpec((tm, 2 * KV_DIM), row)],
        out_shape=[jax.ShapeDtypeStruct((L, 3 * CONV_CH), BF16), jax.ShapeDtypeStruct((L, Q_DIM), BF16),
                   jax.ShapeDtypeStruct((L, 2 * KV_DIM), BF16)],
        compiler_params=_params("parallel"),
        name="in_proj0",
    )(x, g, sc, sh, w, cos, sin)


def _attend(q_ref, keys, vals, bias, sink_ref, o_ref):
    tq = q_ref.shape[0]
    lane = lax.broadcasted_iota(jnp.int32, (tq, LANES), 1)
    low = lane < HEAD_DIM
    for m in range(Q_DIM // LANES):
        qb = q_ref[:, m * LANES:(m + 1) * LANES]
        outs = []
        for half in range(2):
            head = Q_HEAD_ORDER[2 * m + half]
            qh = jnp.where(low if half == 0 else jnp.logical_not(low), qb, jnp.zeros_like(qb))
            s = _dot_nt(qh, keys)
            if bias is not None:
                s = s + bias
            sk = sink_ref[head]
            mx = jnp.maximum(jnp.max(s, axis=-1, keepdims=True), sk)
            p = jnp.exp(s - mx)
            den = jnp.sum(p, axis=-1, keepdims=True) + jnp.exp(sk - mx)
            outs.append(_dot(p.astype(BF16), vals) * (1.0 / den))
        o_ref[:, m * LANES:(m + 1) * LANES] = jnp.where(low, outs[0], outs[1]).astype(BF16)


def _attn_kernel(sink_ref, q_ref, kvp_ref, kvm_ref, kvn_ref, kvc_ref, o_ref, kwin, vwin, *, seq_len):
    i = pl.program_id(0)
    tq = q_ref.shape[0]
    lc = kvc_ref.shape[0]
    w = tq + 2 * WINDOW
    for ref, r0, n in ((kvp_ref, 0, WINDOW), (kvm_ref, WINDOW, tq), (kvn_ref, WINDOW + tq, WINDOW),
                       (kvc_ref, w, lc)):
        kwin[r0:r0 + n] = ref[:, 0:KV_DIM]
        vwin[r0:r0 + n] = ref[:, KV_DIM:]
    r = lax.broadcasted_iota(jnp.int32, (tq, w + lc), 0)
    c = lax.broadcasted_iota(jnp.int32, (tq, w + lc), 1)
    kpos = i * tq - WINDOW + c
    rel = r + WINDOW - c
    bias = jnp.where(c >= w, 0.0,
                     jnp.where(jnp.abs(rel) <= WINDOW,
                               jnp.where(kpos >= 0, jnp.where(kpos < seq_len, 0.0, NEG_BIG), NEG_BIG),
                               NEG_BIG))
    _attend(q_ref, kwin[...], vwin[...], bias, sink_ref, o_ref)


def _attn(sink, q, kv, kvc, tq):
    L = q.shape[0]
    lc = kvc.shape[0]
    nb = L // WINDOW
    per = tq // WINDOW
    w = tq + 2 * WINDOW + lc
    return pl.pallas_call(
        functools.partial(_attn_kernel, seq_len=L),
        grid=(L // tq,),
        in_specs=[pl.BlockSpec(memory_space=pltpu.SMEM),
                  pl.BlockSpec((tq, Q_DIM), lambda i: (i, 0)),
                  pl.BlockSpec((WINDOW, 2 * KV_DIM), lambda i: (jnp.maximum(i * per - 1, 0), 0)),
                  pl.BlockSpec((tq, 2 * KV_DIM), lambda i: (i, 0)),
                  pl.BlockSpec((WINDOW, 2 * KV_DIM), lambda i: (jnp.minimum((i + 1) * per, nb - 1), 0)),
                  _full((lc, 2 * KV_DIM))],
        out_specs=pl.BlockSpec((tq, Q_DIM), lambda i: (i, 0)),
        out_shape=jax.ShapeDtypeStruct((L, Q_DIM), BF16),
        scratch_shapes=[pltpu.VMEM((w, KV_DIM), BF16), pltpu.VMEM((w, KV_DIM), BF16)],
        compiler_params=_params("parallel"),
        name="window_attn",
    )(sink, q, kv, kv, kv, kvc)


def _ctx_attn_kernel(sink_ref, q_ref, kvc_ref, o_ref):
    _attend(q_ref, kvc_ref[:, 0:KV_DIM], kvc_ref[:, KV_DIM:], None, sink_ref, o_ref)


def _ctx_attn(sink, qc, kvc):
    lc = qc.shape[0]
    return pl.pallas_call(
        _ctx_attn_kernel,
        grid=(1,),
        in_specs=[pl.BlockSpec(memory_space=pltpu.SMEM), _full((lc, Q_DIM)), _full((lc, 2 * KV_DIM))],
        out_specs=_full((lc, Q_DIM)),
        out_shape=jax.ShapeDtypeStruct((lc, Q_DIM), BF16),
        compiler_params=_params("arbitrary"),
        name="ctx_attn",
    )(sink, qc, kvc)


def _conv3(u, prev_row, next_row, w_ref, cols):
    tm = u.shape[0]
    rid = lax.broadcasted_iota(jnp.int32, u.shape, 0)
    up = jnp.where(rid == 0, prev_row, pltpu.roll(u, 1, 0))
    un = jnp.where(rid == tm - 1, next_row, pltpu.roll(u, tm - 1, 0))
    return up * w_ref[0:1, cols] + u * w_ref[1:2, cols] + un * w_ref[2:3, cols]


def _halo_specs(tm, n, L):
    per = tm // BF16_SUBLANES
    nb = L // BF16_SUBLANES
    return [pl.BlockSpec((BF16_SUBLANES, n), lambda i: (jnp.maximum(i * per - 1, 0), 0)),
            pl.BlockSpec((BF16_SUBLANES, n), lambda i: (jnp.minimum((i + 1) * per, nb - 1), 0))]


def _out0_kernel(pc_ref, pp_ref, pn_ref, att_ref, x_ref, cw_ref, w_ref, g1_ref, o_ref):
    i = pl.program_id(0)
    nt = pl.num_programs(0)
    gc = slice(CONV_CH, 2 * CONV_CH)
    ha = slice(2 * CONV_CH, 3 * CONV_CH)
    u = pc_ref[:, gc].astype(F32) * pc_ref[:, ha].astype(F32)
    last = BF16_SUBLANES - 1
    pv = pp_ref[last:last + 1, gc].astype(F32) * pp_ref[last:last + 1, ha].astype(F32)
    nx = pn_ref[0:1, gc].astype(F32) * pn_ref[0:1, ha].astype(F32)
    pv = jnp.where(i > 0, pv, 0.0)
    nx = jnp.where(i < nt - 1, nx, 0.0)
    a_out = pc_ref[:, 0:CONV_CH].astype(F32) * _conv3(u, pv, nx, cw_ref, slice(0, CONV_CH))
    y = _dot(a_out.astype(BF16), w_ref[0:CONV_CH, :]) + _dot(att_ref[...], w_ref[CONV_CH:, :])
    o_ref[...] = x_ref[...] + g1_ref[...] * y


def _out0(pc, att, x, conv_w, w_out, g1, tm):
    L, d = x.shape
    row = lambda i: (i, 0)
    return pl.pallas_call(
        _out0_kernel,
        grid=(L // tm,),
        in_specs=[pl.BlockSpec((tm, 3 * CONV_CH), row)] + _halo_specs(tm, 3 * CONV_CH, L) + [
            pl.BlockSpec((tm, Q_DIM), row), pl.BlockSpec((tm, d), row),
            _full(conv_w.shape), _full(w_out.shape), _full((1, d))],
        out_specs=pl.BlockSpec((tm, d), row),
        out_shape=jax.ShapeDtypeStruct((L, d), F32),
        compiler_params=_params("parallel"),
        name="out_proj0",
    )(pc, pc, pc, att, x, conv_w, w_out, g1)


FF_CHUNK = 256


def _ffn_kernel(x_ref, g_ref, sc_ref, sh_ref, g2_ref, wg_ref, wu_ref, wd_ref, o_ref, act_ref):
    x = x_ref[...]
    hb = _modulated(x, g_ref, sc_ref, sh_ref)
    dff = wg_ref.shape[1]
    for c in range(dff // FF_CHUNK):
        sl = slice(c * FF_CHUNK, (c + 1) * FF_CHUNK)
        act_ref[:, sl] = (_silu(_dot(hb, wg_ref[:, sl])) * _dot(hb, wu_ref[:, sl])).astype(BF16)
    o_ref[...] = x + g2_ref[...] * _dot(act_ref[...], wd_ref[...])


def _ffn(x, g, sc, sh, g2, wg, wu, wd, tm):
    L, d = x.shape
    dff = wg.shape[1]
    row = lambda i: (i, 0)
    vec = _full((1, d))
    return pl.pallas_call(
        _ffn_kernel,
        grid=(L // tm,),
        in_specs=[pl.BlockSpec((tm, d), row), vec, vec, vec, vec,
                  _full(wg.shape), _full(wu.shape), _full(wd.shape)],
        out_specs=pl.BlockSpec((tm, d), row),
        out_shape=jax.ShapeDtypeStruct((L, d), F32),
        scratch_shapes=[pltpu.VMEM((tm, dff), BF16)],
        compiler_params=_params("parallel"),
        name="ffn0",
    )(x, g, sc, sh, g2, wg, wu, wd)


def _in1_kernel(x_ref, g_ref, sc_ref, sh_ref, wz_ref, wx_ref, wdt_ref, dtb_ref, z_ref, xbc_ref, dt_ref):
    hb = _modulated(x_ref[...], g_ref, sc_ref, sh_ref)
    for c in range(wz_ref.shape[1] // 512):
        sl = slice(c * 512, (c + 1) * 512)
        z_ref[:, sl] = _dot(hb, wz_ref[:, sl]).astype(BF16)
    for c in range(wx_ref.shape[1] // 512):
        sl = slice(c * 512, (c + 1) * 512)
        xbc_ref[:, sl] = _dot(hb, wx_ref[:, sl]).astype(BF16)
    d = _dot(hb, wdt_ref[...]) + dtb_ref[...]
    dt_ref[...] = jnp.maximum(d, 0.0) + jnp.log(1.0 + jnp.exp(-jnp.abs(d)))


def _in1(x, g, sc, sh, wz, wx, wdt, dtb, tm):
    L, d = x.shape
    row = lambda i: (i, 0)
    vec = _full((1, d))
    nz, nx, nd = wz.shape[1], wx.shape[1], wdt.shape[1]
    return pl.pallas_call(
        _in1_kernel,
        grid=(L // tm,),
        in_specs=[pl.BlockSpec((tm, d), row), vec, vec, vec,
                  _full(wz.shape), _full(wx.shape), _full(wdt.shape), _full((1, nd))],
        out_specs=[pl.BlockSpec((tm, nz), row), pl.BlockSpec((tm, nx), row), pl.BlockSpec((tm, nd), row)],
        out_shape=[jax.ShapeDtypeStruct((L, nz), BF16), jax.ShapeDtypeStruct((L, nx), BF16),
                   jax.ShapeDtypeStruct((L, nd), F32)],
        compiler_params=_params("parallel"),
        name="in_proj1",
    )(x, g, sc, sh, wz, wx, wdt, dtb)


def _conv1_kernel(xm_ref, xp_ref, xn_ref, w_ref, b_ref, o_ref):
    i = pl.program_id(0)
    nt = pl.num_programs(0)
    last = BF16_SUBLANES - 1
    for c in range(xm_ref.shape[1] // 512):
        sl = slice(c * 512, (c + 1) * 512)
        u = xm_ref[:, sl].astype(F32)
        pv = jnp.where(i > 0, xp_ref[last:last + 1, sl].astype(F32), 0.0)
        nx = jnp.where(i < nt - 1, xn_ref[0:1, sl].astype(F32), 0.0)
        y = _conv3(u, pv, nx, w_ref, sl) + b_ref[:, sl]
        o_ref[:, sl] = _silu(y).astype(BF16)


def _conv1(xbc, conv_w, conv_b, tm):
    L, n = xbc.shape
    row = lambda i: (i, 0)
    return pl.pallas_call(
        _conv1_kernel,
        grid=(L // tm,),
        in_specs=[pl.BlockSpec((tm, n), row)] + _halo_specs(tm, n, L) + [_full(conv_w.shape), _full((1, n))],
        out_specs=pl.BlockSpec((tm, n), row),
        out_shape=jax.ShapeDtypeStruct((L, n), BF16),
        compiler_params=_params("parallel"),
        name="ssd_conv",
    )(xbc, xbc, xbc, conv_w, conv_b)


def _ssd_kernel(x_ref, b_ref, c_ref, dt_ref, alog_ref, dexp_ref, e_ref, init_ref, y_ref, fin_ref, st_ref,
                *, reverse, dcol):
    i = pl.program_id(0)
    nc = pl.num_programs(0)
    q = SSD_CHUNK
    hpg = SSD_HEADS // SSD_GROUPS
    gw = D_INNER // SSD_GROUPS

    @pl.when(i == 0)
    def _():
        st_ref[...] = init_ref[...]

    x = x_ref[...]
    dt = dt_ref[:, dcol:dcol + SSD_HEADS]
    a = dt * (-jnp.exp(alog_ref[...]))
    rr = lax.broadcasted_iota(jnp.int32, (q, q), 0)
    cc = lax.broadcasted_iota(jnp.int32, (q, q), 1)
    tri = (cc >= rr) if reverse else (cc <= rr)
    tmat = jnp.where(tri, 1.0, 0.0).astype(BF16)
    a_hi, a_lo = _split2(a)
    cum = _dot(tmat, a_hi) + _dot(tmat, a_lo)
    total = cum[0:1] if reverse else cum[q - 1:q]
    din = jnp.exp(cum)
    wgt = dt * jnp.exp(total - cum)
    cdec = jnp.broadcast_to(jnp.exp(total), (8, SSD_HEADS))

    emat = e_ref[...]

    def expand(v):
        hi, lo = _split2(v)
        return _dot(hi, emat) + _dot(lo, emat)

    din_e = expand(din)
    wgt_e = expand(wgt)
    cdec_e = expand(cdec)[0:1]

    eye = jnp.where(lax.broadcasted_iota(jnp.int32, (SSD_HEADS, SSD_HEADS), 0)
                    == lax.broadcasted_iota(jnp.int32, (SSD_HEADS, SSD_HEADS), 1), 1.0, 0.0).astype(BF16)

    def to_rows(v):
        hi, lo = _split2(v)
        return _dot_nt(eye, hi) + _dot_nt(eye, lo)

    cum_t = to_rows(cum)
    dt_t = to_rows(dt)
    xw = (x.astype(F32) * wgt_e).astype(BF16)
    lane = lax.broadcasted_iota(jnp.int32, (q, LANES), 1)
    low = lane < HEAD_DIM

    for g in range(SSD_GROUPS):
        bg = b_ref[:, g * SSD_STATE:(g + 1) * SSD_STATE]
        cg = c_ref[:, g * SSD_STATE:(g + 1) * SSD_STATE]
        cb = _dot_nt(cg, bg)
        st = st_ref[g]
        gs = slice(g * gw, (g + 1) * gw)
        y_off = _dot(cg, st.astype(BF16)) * din_e[:, gs]
        for pr in range(hpg // 2):
            ms = []
            for hh in range(2):
                h = g * hpg + 2 * pr + hh
                diff = cum[:, h:h + 1] - cum_t[h:h + 1, :]
                lm = jnp.exp(jnp.where(tri, diff, NEG_BIG))
                ms.append((cb * lm * dt_t[h:h + 1, :]).astype(BF16))
            col = g * gw + pr * LANES
            xp = x[:, col:col + LANES]
            zero = jnp.zeros_like(xp)
            rhs = jnp.concatenate([jnp.where(low, xp, zero), jnp.where(low, zero, xp)], axis=0)
            y = _dot(jnp.concatenate(ms, axis=1), rhs) + y_off[:, pr * LANES:(pr + 1) * LANES]
            if not reverse:
                y = y + xp.astype(F32) * dexp_ref[:, col:col + LANES]
            y_ref[:, col:col + LANES] = y.astype(BF16)
        st_ref[g] = st * cdec_e[:, gs] + _dot_tn(bg, xw[:, gs])

    @pl.when(i == nc - 1)
    def _():
        fin_ref[...] = st_ref[...]


def _ssd(xact, dt, alog, dexp, emat, init, reverse):
    L = xact.shape[0]
    q = SSD_CHUNK
    nc = L // q
    chunk = (lambda i: nc - 1 - i) if reverse else (lambda i: i)
    st_shape = (SSD_GROUPS, SSD_STATE, D_INNER // SSD_GROUPS)
    return pl.pallas_call(
        functools.partial(_ssd_kernel, reverse=reverse, dcol=SSD_HEADS if reverse else 0),
        grid=(nc,),
        in_specs=[pl.BlockSpec((q, D_INNER), lambda i: (chunk(i), 0)),
                  pl.BlockSpec((q, GN_DIM), lambda i: (chunk(i), D_INNER // GN_DIM)),
                  pl.BlockSpec((q, GN_DIM), lambda i: (chunk(i), D_INNER // GN_DIM + 1)),
                  pl.BlockSpec((q, 2 * SSD_HEADS), lambda i: (chunk(i), 0)),
                  _full((1, SSD_HEADS)), _full((1, D_INNER)), _full(emat.shape), _full(st_shape)],
        out_specs=[pl.BlockSpec((q, D_INNER), lambda i: (chunk(i), 0)), _full(st_shape)],
        out_shape=[jax.ShapeDtypeStruct((L, D_INNER), BF16), jax.ShapeDtypeStruct(st_shape, F32)],
        scratch_shapes=[pltpu.VMEM(st_shape, F32)],
        compiler_params=_params("arbitrary"),
        name="ssd_bwd" if reverse else "ssd_fwd",
    )(xact, xact, xact, dt, alog, dexp, emat, init)


def _out1_kernel(yf_ref, yb_ref, z_ref, x_ref, ng_ref, w_ref, g1_ref, o_ref, yn_ref):
    gw = D_INNER // SSD_GROUPS
    for g in range(SSD_GROUPS):
        sl = slice(g * gw, (g + 1) * gw)
        y = (yf_ref[:, sl].astype(F32) + yb_ref[:, sl].astype(F32)) * _silu(z_ref[:, sl].astype(F32))
        y = y * lax.rsqrt(jnp.mean(y * y, axis=-1, keepdims=True) + EPS)
        yn_ref[:, sl] = (y * ng_ref[:, sl]).astype(BF16)
    o_ref[...] = x_ref[...] + g1_ref[...] * _dot(yn_ref[...], w_ref[...])


def _out1(yf, yb, z, x, norm_g, w_out, g1, tm):
    L, d = x.shape
    row = lambda i: (i, 0)
    wide = pl.BlockSpec((tm, D_INNER), row)
    return pl.pallas_call(
        _out1_kernel,
        grid=(L // tm,),
        in_specs=[wide, wide, wide, pl.BlockSpec((tm, d), row), _full((1, D_INNER)), _full(w_out.shape),
                  _full((1, d))],
        out_specs=pl.BlockSpec((tm, d), row),
        out_shape=jax.ShapeDtypeStruct((L, d), F32),
        scratch_shapes=[pltpu.VMEM((tm, D_INNER), BF16)],
        compiler_params=_params("parallel"),
        name="out_proj1",
    )(yf, yb, z, x, norm_g, w_out, g1)


def _router_kernel(x_ref, g_ref, sc_ref, sh_ref, r_ref, h_ref, gate_ref):
    hb = _modulated(x_ref[...], g_ref, sc_ref, sh_ref)
    h_ref[...] = hb
    logits = _dot(hb, r_ref[...])
    e = lax.broadcasted_iota(jnp.int32, logits.shape, 1)
    m1 = jnp.max(logits, axis=-1, keepdims=True)
    i1 = jnp.min(jnp.where(logits == m1, e, N_EXPERTS), axis=-1, keepdims=True)
    rest = jnp.where(e == i1, -jnp.inf, logits)
    m2 = jnp.max(rest, axis=-1, keepdims=True)
    i2 = jnp.min(jnp.where(rest == m2, e, N_EXPERTS), axis=-1, keepdims=True)
    p2 = jnp.exp(m2 - m1)
    den = 1.0 + p2
    gate_ref[...] = jnp.where(e == i1, 1.0 / den, jnp.where(e == i2, p2 / den, 0.0))


def _router(x, g, sc, sh, router, tm):
    L, d = x.shape
    row = lambda i: (i, 0)
    vec = _full((1, d))
    return pl.pallas_call(
        _router_kernel,
        grid=(L // tm,),
        in_specs=[pl.BlockSpec((tm, d), row), vec, vec, vec, _full(router.shape)],
        out_specs=[pl.BlockSpec((tm, d), row), pl.BlockSpec((tm, N_EXPERTS), row)],
        out_shape=[jax.ShapeDtypeStruct((L, d), BF16), jax.ShapeDtypeStruct((L, N_EXPERTS), F32)],
        compiler_params=_params("parallel"),
        name="router",
    )(x, g, sc, sh, router)


def _moe_kernel(h_ref, gate_ref, x_ref, g2_ref, fg_ref, wg_ref, wu_ref, wd_ref, o_ref, acc_ref, act_ref):
    e = pl.program_id(1)
    j = pl.program_id(2)
    ne = pl.num_programs(1)
    nj = pl.num_programs(2)

    @pl.when((e == 0) & (j == 0))
    def _():
        acc_ref[...] = jnp.zeros_like(acc_ref)

    hb = h_ref[...]
    gates = gate_ref[...]
    lane = lax.broadcasted_iota(jnp.int32, gates.shape, 1)
    gcol = jnp.sum(jnp.where(lane == e, gates, 0.0), axis=-1, keepdims=True)
    nf = wg_ref.shape[2]
    for c in range(nf // FF_CHUNK):
        sl = slice(c * FF_CHUNK, (c + 1) * FF_CHUNK)
        act = _silu(_dot(hb, wg_ref[0, :, sl])) * _dot(hb, wu_ref[0, :, sl])
        act_ref[:, sl] = (act * gcol).astype(BF16)
    acc_ref[...] += _dot(act_ref[...], wd_ref[0])

    @pl.when((e == ne - 1) & (j == nj - 1))
    def _():
        y = x_ref[...] + g2_ref[...] * acc_ref[...]
        o_ref[...] = y * lax.rsqrt(jnp.mean(y * y, axis=-1, keepdims=True) + EPS) * fg_ref[...]


def _moe(h, gates, x, g2, final_g, wg, wu, wd, tm):
    L, d = x.shape
    ne, _, dff = wg.shape
    nj = 2
    nf = dff // nj
    row = lambda i, e, j: (i, 0)
    vec = pl.BlockSpec((1, d), lambda i, e, j: (0, 0))
    return pl.pallas_call(
        _moe_kernel,
        grid=(L // tm, ne, nj),
        in_specs=[pl.BlockSpec((tm, d), row), pl.BlockSpec((tm, N_EXPERTS), row), pl.BlockSpec((tm, d), row),
                  vec, vec,
                  pl.BlockSpec((1, d, nf), lambda i, e, j: (e, 0, j)),
                  pl.BlockSpec((1, d, nf), lambda i, e, j: (e, 0, j)),
                  pl.BlockSpec((1, nf, d), lambda i, e, j: (e, j, 0))],
        out_specs=pl.BlockSpec((tm, d), row),
        out_shape=jax.ShapeDtypeStruct((L, d), F32),
        scratch_shapes=[pltpu.VMEM((tm, d), F32), pltpu.VMEM((tm, nf), BF16)],
        compiler_params=_params("parallel", "arbitrary", "arbitrary"),
        name="moe",
    )(h, gates, x, g2, final_g, wg, wu, wd)


def _rope_tables(L):
    rows = L // GRID_W
    row = jnp.repeat(jnp.arange(rows, dtype=F32), GRID_W)
    col = jnp.tile(jnp.arange(GRID_W, dtype=F32), rows)
    inv = ROPE_BASE ** (-jnp.arange(ROPE_PAIRS, dtype=F32) / ROPE_PAIRS)
    ar = row[:, None] * inv
    ac = col[:, None] * inv
    cos = jnp.concatenate([jnp.cos(ar), jnp.cos(ar), jnp.cos(ac), jnp.cos(ac)], axis=-1)
    sin = jnp.concatenate([-jnp.sin(ar), jnp.sin(ar), -jnp.sin(ac), jnp.sin(ac)], axis=-1)
    return jnp.tile(cos, (1, LANES // HEAD_DIM)), jnp.tile(sin, (1, LANES // HEAD_DIM))


def _head_perm(offset):
    return jnp.concatenate([offset + h * HEAD_DIM + jnp.arange(HEAD_DIM) for h in Q_HEAD_ORDER])


def kernel(x, c, ctx, c_ctx, ada_w, ada_b, norm1_g, norm2_g, ab_w_in, ab_conv_w, ab_sink, ab_w_out,
           ffn_w_gate, ffn_w_up, ffn_w_down, ssd_w_in, ssd_conv_w, ssd_conv_b, ssd_dt_bias_f,
           ssd_dt_bias_b, ssd_a_log_f, ssd_a_log_b, ssd_d, ssd_norm_g, ssd_w_out, moe_router,
           moe_w_gate, moe_w_up, moe_w_down, final_g):
    assert x.shape[0] == 1 and c.shape[0] == 1
    L, d = x.shape[1], x.shape[2]
    lc = ctx.shape[1]
    xl = x[0]
    xc = ctx[0]
    tm = min(512, L)
    tmc = min(512, lc)

    cv = jnp.zeros((8, d), F32).at[0].set(c[0]).at[1].set(c_ctx)
    mod = _ada(cv, ada_w, ada_b)

    def mods(i, r):
        return [_row(mod[i, r, k * d:(k + 1) * d]) for k in range(6)]

    sh1, sc1, g1, sh2, sc2, g2 = mods(0, 0)
    sh1c, sc1c, g1c, sh2c, sc2c, g2c = mods(0, 1)
    q0 = 3 * CONV_CH
    w_in = ab_w_in[0]
    w_in = jnp.concatenate([w_in[:, :q0], w_in[:, _head_perm(q0)], w_in[:, q0 + Q_DIM:]], axis=1).astype(BF16)
    w_out = ab_w_out[0]
    w_out = jnp.concatenate([w_out[:CONV_CH], w_out[_head_perm(CONV_CH)]], axis=0).astype(BF16)
    cos, sin = _rope_tables(L)
    ones = jnp.ones((lc, LANES), F32)
    n1 = _row(norm1_g[0])
    n2 = _row(norm2_g[0])
    conv_w0 = ab_conv_w[0]
    sink = ab_sink[0]
    wg0, wu0, wd0 = ffn_w_gate[0].astype(BF16), ffn_w_up[0].astype(BF16), ffn_w_down[0].astype(BF16)

    pcc, qc, kvc = _in0(xc, n1, sc1c, sh1c, w_in, ones, jnp.zeros_like(ones), tmc)
    pc, q, kv = _in0(xl, n1, sc1, sh1, w_in, cos, sin, tm)
    att = _attn(sink, q, kv, kvc, min(256, L))
    attc = _ctx_attn(sink, qc, kvc)
    xl = _out0(pc, att, xl, conv_w0, w_out, g1, tm)
    xc = _out0(pcc, attc, xc, conv_w0, w_out, g1c, tmc)
    xl = _ffn(xl, n2, sc2, sh2, g2, wg0, wu0, wd0, tm)
    xc = _ffn(xc, n2, sc2c, sh2c, g2c, wg0, wu0, wd0, tmc)

    sh1, sc1, g1, sh2, sc2, g2 = mods(1, 0)
    sh1c, sc1c, _, _, _, _ = mods(1, 1)
    n1 = _row(norm1_g[1])
    n2 = _row(norm2_g[1])
    w1 = ssd_w_in[0]
    conv_dim = D_INNER + 2 * GN_DIM
    wz = w1[:, :D_INNER].astype(BF16)
    wx = w1[:, D_INNER:D_INNER + conv_dim].astype(BF16)
    wdt = w1[:, D_INNER + conv_dim:].astype(BF16)
    dtb = _row(jnp.concatenate([ssd_dt_bias_f[0], ssd_dt_bias_b[0]]))
    conv_w1 = ssd_conv_w[0]
    conv_b1 = _row(ssd_conv_b[0])
    alog_f = _row(ssd_a_log_f[0])
    alog_b = _row(ssd_a_log_b[0])
    dexp = _row(jnp.repeat(ssd_d[0], HEAD_DIM))
    emat = jnp.repeat(jnp.eye(SSD_HEADS, dtype=BF16), HEAD_DIM, axis=1)
    zero_state = jnp.zeros((SSD_GROUPS, SSD_STATE, D_INNER // SSD_GROUPS), F32)

    _, xbc_c, dt_c = _in1(xc, n1, sc1c, sh1c, wz, wx, wdt, dtb, tmc)
    xact_c = _conv1(xbc_c, conv_w1, conv_b1, tmc)
    _, s_f = _ssd(xact_c, dt_c, alog_f, dexp, emat, zero_state, False)
    _, s_b = _ssd(xact_c, dt_c, alog_b, dexp, emat, zero_state, True)

    z, xbc, dt = _in1(xl, n1, sc1, sh1, wz, wx, wdt, dtb, tm)
    xact = _conv1(xbc, conv_w1, conv_b1, tm)
    y_f, _ = _ssd(xact, dt, alog_f, dexp, emat, s_f, False)
    y_b, _ = _ssd(xact, dt, alog_b, dexp, emat, s_b, True)
    xl = _out1(y_f, y_b, z, xl, _row(ssd_norm_g[0]), ssd_w_out[0].astype(BF16), g1, tm)

    h2, gates = _router(xl, n2, sc2, sh2, moe_router[0].astype(BF16), tm)
    out = _moe(h2, gates, xl, g2, _row(final_g), moe_w_gate[0].astype(BF16), moe_w_up[0].astype(BF16),
               moe_w_down[0].astype(BF16), tm)
    return out[None]
```
